```python
import jax, jax.numpy as jnp
from jax import lax
import numpy as np

D_MODEL = 2048
BATCH = 4
SEQ = 2048
DEPTH = 1

CHUNK = 64
Q_BLOCK = 128
D_CONV = D_MODEL // 2
CONV_GROUPS = 8
CONV_K = 3
QK_NOPE = 128
QK_ROPE = 64
V_HEAD = 128
MLA_HEADS = (D_MODEL // 2) // V_HEAD
D_ATTN_OUT = MLA_HEADS * V_HEAD
Q_LORA = D_MODEL // 4
KV_LORA = D_MODEL // 4
ROPE_THETA = 10000.0
D_MIX = D_CONV + D_ATTN_OUT
IN_SPLIT_SIZES = (D_CONV, D_CONV, D_CONV, Q_LORA, KV_LORA, QK_ROPE)
IN_COLS = sum(IN_SPLIT_SIZES)
PEER_HEADS = 8
PEER_N_KEYS = 128
PEER_N_EXPERTS = PEER_N_KEYS * PEER_N_KEYS
PEER_QDIM = 256
PEER_TOPK = 16
PEER_TOKEN_BLOCK = 128
N_MOD = 6
EPS = 1e-6
NEG_INF = -1e30

kernel_name = 'hybrid_conv_mla_peer_adaln_block'


def rms_norm(x, g):
    x32 = x.astype(jnp.float32)
    y = x32 * lax.rsqrt(jnp.mean(x32 * x32, axis=-1, keepdims=True) + EPS)
    return (y * g.astype(jnp.float32)).astype(x.dtype)


def group_rms_norm(x, g, n_groups):
    shp = x.shape
    xg = x.reshape(shp[:-1] + (n_groups, shp[-1] // n_groups)).astype(jnp.float32)
    y = xg * lax.rsqrt(jnp.mean(xg * xg, axis=-1, keepdims=True) + EPS)
    return (y.reshape(shp) * g.astype(jnp.float32)).astype(x.dtype)


def rope_tables(seq, dim):
    inv = 1.0 / (ROPE_THETA ** (jnp.arange(0, dim, 2, dtype=jnp.float32) / dim))
    ang = jnp.arange(seq, dtype=jnp.float32)[:, None] * inv[None, :]
    return jnp.cos(ang), jnp.sin(ang)


def apply_rope(x, cos, sin):
    x32 = x.astype(jnp.float32)
    x1, x2 = jnp.split(x32, 2, axis=-1)
    out = jnp.concatenate([x1 * cos - x2 * sin, x2 * cos + x1 * sin], axis=-1)
    return out.astype(x.dtype)


def short_conv_mixer(b_gate, c_gate, h, conv_w):
    z = c_gate * h
    seq = z.shape[1]
    zp = jnp.pad(z, ((0, 0), (CONV_K - 1, 0), (0, 0)))
    y = conv_w[0] * zp[:, 0:seq]
    for k in range(1, CONV_K):
        y = y + conv_w[k] * zp[:, k:k + seq]
    return b_gate * y


def mla_attention(q_lat, kv_lat, k_rope_raw, g_q_lat, w_uq, g_kv_lat, w_ukv):
    bsz, seq, _ = q_lat.shape
    q = (rms_norm(q_lat, g_q_lat) @ w_uq).reshape(bsz, seq, MLA_HEADS, QK_NOPE + QK_ROPE)
    q_nope, q_rope = q[..., :QK_NOPE], q[..., QK_NOPE:]
    kv = (rms_norm(kv_lat, g_kv_lat) @ w_ukv).reshape(bsz, seq, MLA_HEADS, QK_NOPE + V_HEAD)
    k_nope, v = kv[..., :QK_NOPE], kv[..., QK_NOPE:]
    cos, sin = rope_tables(seq, QK_ROPE)
    q_rope = apply_rope(q_rope, cos[None, :, None, :], sin[None, :, None, :])
    k_rope = apply_rope(k_rope_raw, cos[None], sin[None])
    scale = (QK_NOPE + QK_ROPE) ** -0.5
    chunk_id = jnp.arange(seq) // CHUNK
    outs = []
    for blk in range(seq // Q_BLOCK):
        q0, q1 = blk * Q_BLOCK, (blk + 1) * Q_BLOCK
        s = (jnp.einsum('bqhd,bkhd->bhqk', q_nope[:, q0:q1], k_nope[:, :q1])
             + jnp.einsum('bqhr,bkr->bhqk', q_rope[:, q0:q1], k_rope[:, :q1])).astype(jnp.float32) * scale
        mask = chunk_id[None, :q1] <= chunk_id[q0:q1, None]
        s = jnp.where(mask, s, NEG_INF)
        p = jax.nn.softmax(s, axis=-1).astype(v.dtype)
        outs.append(jnp.einsum('bhqk,bkhd->bqhd', p, v[:, :q1]))
    o = jnp.concatenate(outs, axis=1)
    return o.reshape(bsz, seq, D_ATTN_OUT)


def peer_ffn(h, w_q, sub_keys, u_tab, v_tab):
    bsz, seq, d = h.shape
    q = (h @ w_q).reshape(bsz, seq, PEER_HEADS, 2, PEER_QDIM // 2)
    scores = jnp.einsum('bshpd,hpnd->bshpn', q, sub_keys).astype(jnp.float32)
    top_v, top_i = lax.top_k(scores, PEER_TOPK)
    cand = top_v[..., 0, :, None] + top_v[..., 1, None, :]
    cand = cand.reshape(bsz, seq, PEER_HEADS, PEER_TOPK * PEER_TOPK)
    best_v, best_i = lax.top_k(cand, PEER_TOPK)
    i1 = jnp.take_along_axis(top_i[..., 0, :], best_i // PEER_TOPK, axis=-1)
    i2 = jnp.take_along_axis(top_i[..., 1, :], best_i % PEER_TOPK, axis=-1)
    expert = i1 * PEER_N_KEYS + i2
    gate = jax.nn.softmax(best_v, axis=-1).astype(h.dtype)
    n_sel = PEER_HEADS * PEER_TOPK
    n_blk = (bsz * seq) // PEER_TOKEN_BLOCK
    hb = h.reshape(n_blk, PEER_TOKEN_BLOCK, d)
    eb = expert.reshape(n_blk, PEER_TOKEN_BLOCK, n_sel)
    gb = gate.reshape(n_blk, PEER_TOKEN_BLOCK, n_sel)

    def token_block(args):
        hx, ids, g = args
        u = u_tab[ids]
        a = jnp.einsum('td,tnd->tn', hx, u)
        act = jax.nn.gelu(a, approximate=False) * g
        return jnp.einsum('tn,tnd->td', act, v_tab[ids])

    out = lax.map(token_block, (hb, eb, gb))
    return out.reshape(bsz, seq, d)


def setup_inputs(seed: int = 0) -> dict:
    key = jax.random.key(seed)
    ks = jax.random.split(key, 20)
    f32 = jnp.float32
    nrm = lambda k, shape, s: jax.random.normal(k, shape, f32) * s
    gain = lambda k, shape: 1.0 + 0.02 * jax.random.normal(k, shape, f32)
    L = DEPTH
    return {
        'x': jax.random.normal(ks[0], (BATCH, SEQ, D_MODEL), f32),
        'c': jax.random.normal(ks[1], (BATCH, D_MODEL), f32),
        'w_ada': nrm(ks[2], (L, D_MODEL, N_MOD * D_MODEL), 0.5 * D_MODEL ** -0.5),
        'b_ada': nrm(ks[3], (L, N_MOD * D_MODEL), 0.02),
        'g_norm_mix': gain(ks[4], (L, D_MODEL)),
        'w_in': nrm(ks[5], (L, D_MODEL, IN_COLS), D_MODEL ** -0.5),
        'conv_w': nrm(ks[6], (L, CONV_K, D_CONV), CONV_K ** -0.5),
        'g_q_lat': gain(ks[7], (L, Q_LORA)),
        'w_uq': nrm(ks[8], (L, Q_LORA, MLA_HEADS * (QK_NOPE + QK_ROPE)), Q_LORA ** -0.5),
        'g_kv_lat': gain(ks[9], (L, KV_LORA)),
        'w_ukv': nrm(ks[10], (L, KV_LORA, MLA_HEADS * (QK_NOPE + V_HEAD)), KV_LORA ** -0.5),
        'g_out_conv': gain(ks[11], (L, D_CONV)),
        'g_out_attn': gain(ks[12], (L, D_ATTN_OUT)),
        'w_out': nrm(ks[13], (L, D_MIX, D_MODEL), D_MIX ** -0.5),
        'g_norm_ffn': gain(ks[14], (L, D_MODEL)),
        'peer_w_q': nrm(ks[15], (L, D_MODEL, PEER_HEADS * PEER_QDIM), D_MODEL ** -0.5),
        'peer_sub_keys': nrm(ks[16], (L, PEER_HEADS, 2, PEER_N_KEYS, PEER_QDIM // 2), (PEER_QDIM // 2) ** -0.5),
        'peer_u': nrm(ks[17], (L, PEER_N_EXPERTS, D_MODEL), D_MODEL ** -0.5),
        'peer_v': nrm(ks[18], (L, PEER_N_EXPERTS, D_MODEL), PEER_HEADS ** -0.5),
        'g_final': gain(ks[19], (D_MODEL,)),
    }


def reference(x, c, w_ada, b_ada, g_norm_mix, w_in, conv_w, g_q_lat, w_uq, g_kv_lat, w_ukv,
              g_out_conv, g_out_attn, w_out, g_norm_ffn, peer_w_q, peer_sub_keys, peer_u, peer_v, g_final):
    split_at = np.cumsum(IN_SPLIT_SIZES)[:-1].tolist()
    c_act = jax.nn.silu(c)
    for l in range(DEPTH):
        mod = c_act @ w_ada[l] + b_ada[l]
        sh_m, sc_m, gt_m, sh_f, sc_f, gt_f = [m[:, None, :] for m in jnp.split(mod, N_MOD, axis=-1)]
        h = rms_norm(x, g_norm_mix[l]) * (1 + sc_m) + sh_m
        proj = h @ w_in[l]
        b_g, c_g, h_c, q_lat, kv_lat, k_rope_raw = jnp.split(proj, split_at, axis=-1)
        conv_out = short_conv_mixer(b_g, c_g, h_c, conv_w[l])
        attn_out = mla_attention(q_lat, kv_lat, k_rope_raw, g_q_lat[l], w_uq[l],
                                 g_kv_lat[l], w_ukv[l])
        merged = jnp.concatenate([group_rms_norm(conv_out, g_out_conv[l], CONV_GROUPS),
                                  group_rms_norm(attn_out, g_out_attn[l], MLA_HEADS)], axis=-1)
        x = x + gt_m * (merged @ w_out[l])
        h2 = rms_norm(x, g_norm_ffn[l]) * (1 + sc_f) + sh_f
        x = x + gt_f * peer_ffn(h2, peer_w_q[l], peer_sub_keys[l], peer_u[l], peer_v[l])
    return rms_norm(x, g_final)
```

```python
import functools
import math

import jax
import jax.numpy as jnp
from jax import lax
from jax.experimental import pallas as pl
from jax.experimental.pallas import tpu as pltpu

F32 = jnp.float32
BF16 = jnp.bfloat16

EPS = 1e-6
MASK_VALUE = -1e30
CHUNK = 64
CONV_GROUPS = 8
CONV_K = 3
QK_NOPE = 128
QK_ROPE = 64
V_HEAD = 128
ROPE_THETA = 10000.0
PEER_HEADS = 8
PEER_TOPK = 16
N_MOD = 6

LANES = 128
SUBLANES = 8
VMEM_LIMIT_BYTES = 56 * 1024 * 1024


def _cparams(semantics):
    return pltpu.CompilerParams(dimension_semantics=semantics, vmem_limit_bytes=VMEM_LIMIT_BYTES)


def _const_spec(shape):
    nd = len(shape)
    return pl.BlockSpec(shape, lambda *_: (0,) * nd, pipeline_mode=pl.Buffered(1))


def _adaln_kernel(c_ref, w_ref, b_ref, o_ref):
    c = c_ref[...]
    act = (c * jax.nn.sigmoid(c)).astype(BF16)
    o_ref[...] = jnp.dot(act, w_ref[...].astype(BF16), preferred_element_type=F32) + b_ref[...]


def _adaln(c_pad, w_ada, b_ada, tn):
    bp, d = c_pad.shape
    n = w_ada.shape[1]
    return pl.pallas_call(
        _adaln_kernel,
        grid=(n // tn,),
        in_specs=[
            pl.BlockSpec((bp, d), lambda j: (0, 0)),
            pl.BlockSpec((d, tn), lambda j: (0, j)),
            pl.BlockSpec((1, tn), lambda j: (0, j)),
        ],
        out_specs=pl.BlockSpec((bp, tn), lambda j: (0, j)),
        out_shape=jax.ShapeDtypeStruct((bp, n), F32),
        compiler_params=_cparams(("arbitrary",)),
        name="adaln",
    )(c_pad, w_ada, b_ada)


def _mix_in_kernel(x_ref, mod_ref, gn_ref, win_ref, convw_ref, goc_ref, gq_ref, gkv_ref, wuq_ref, wukv_ref,
                   cos_ref, sin_ref, conv_o, q_o, k_o, v_o, zbuf, *, nb_per_seq, dconv, qlora, kvlora, heads, scale):
    i = pl.program_id(0)
    tm = x_ref.shape[0]
    x = x_ref[...]
    ms = jnp.mean(x * x, axis=-1, keepdims=True)
    h = x * lax.rsqrt(ms + EPS) * gn_ref[...] * (1.0 + mod_ref[1:2, :]) + mod_ref[0:1, :]
    hb = h.astype(BF16)

    def proj(a, b):
        return jnp.dot(hb, win_ref[:, a:b], preferred_element_type=F32)

    @pl.when(i % nb_per_seq == 0)
    def _():
        zbuf[0:SUBLANES, :] = jnp.zeros((SUBLANES, dconv), F32)

    z = proj(dconv, 2 * dconv) * proj(2 * dconv, 3 * dconv)
    zbuf[SUBLANES:SUBLANES + tm, :] = z
    y = convw_ref[CONV_K - 1:CONV_K, :] * z
    for k in range(CONV_K - 1):
        back = CONV_K - 1 - k
        y = y + convw_ref[k:k + 1, :] * zbuf[SUBLANES - back:SUBLANES - back + tm, :]
    zbuf[0:SUBLANES, :] = zbuf[tm:tm + SUBLANES, :]
    co = proj(0, dconv) * y
    gs = dconv // CONV_GROUPS
    for g in range(CONV_GROUPS):
        blk = co[:, g * gs:(g + 1) * gs]
        msg = jnp.mean(blk * blk, axis=-1, keepdims=True)
        conv_o[:, g * gs:(g + 1) * gs] = (blk * lax.rsqrt(msg + EPS) * goc_ref[:, g * gs:(g + 1) * gs]).astype(BF16)

    base = 3 * dconv
    cos = cos_ref[...]
    sin = sin_ref[...]

    qlat = proj(base, base + qlora)
    msq = jnp.mean(qlat * qlat, axis=-1, keepdims=True)
    hq = (qlat * lax.rsqrt(msq + EPS) * gq_ref[...]).astype(BF16)
    hn = heads * QK_NOPE
    q_nope = jnp.dot(hq, wuq_ref[:, 0:hn], preferred_element_type=F32)
    q_rope = jnp.dot(hq, wuq_ref[:, hn:2 * hn], preferred_element_type=F32)
    q_rot = jnp.dot(hq, wuq_ref[:, 2 * hn:3 * hn], preferred_element_type=F32)

    kvlat = proj(base + qlora, base + qlora + kvlora)
    mskv = jnp.mean(kvlat * kvlat, axis=-1, keepdims=True)
    hkv = (kvlat * lax.rsqrt(mskv + EPS) * gkv_ref[...]).astype(BF16)
    k_nope = jnp.dot(hkv, wukv_ref[:, 0:hn], preferred_element_type=F32)
    v_o[...] = jnp.dot(hkv, wukv_ref[:, hn:2 * hn], preferred_element_type=F32).astype(BF16)

    kb = base + qlora + kvlora
    k_rope = (proj(kb, kb + LANES) * cos + proj(kb + LANES, kb + 2 * LANES) * sin).astype(BF16)

    hd = 2 * LANES
    for hh in range(heads):
        sl = slice(hh * LANES, (hh + 1) * LANES)
        q_o[:, hh * hd:hh * hd + LANES] = (q_nope[:, sl] * scale).astype(BF16)
        q_o[:, hh * hd + LANES:(hh + 1) * hd] = ((q_rope[:, sl] * cos + q_rot[:, sl] * sin) * scale).astype(BF16)
        k_o[:, hh * hd:hh * hd + LANES] = k_nope[:, sl].astype(BF16)
        k_o[:, hh * hd + LANES:(hh + 1) * hd] = k_rope


def _attn_kernel(q_ref, k_ref, v_ref, g_ref, o_ref, *, qb):
    seq = q_ref.shape[0]
    shift = int(math.log2(CHUNK))
    for blk in range(seq // qb):
        q0, q1 = blk * qb, (blk + 1) * qb
        s = lax.dot_general(q_ref[q0:q1, :], k_ref[0:q1, :], (((1,), (1,)), ((), ())), preferred_element_type=F32)
        row_chunk = lax.shift_right_logical(q0 + lax.broadcasted_iota(jnp.int32, (qb, q1), 0), shift)
        col_chunk = lax.shift_right_logical(lax.broadcasted_iota(jnp.int32, (qb, q1), 1), shift)
        s = jnp.where(col_chunk <= row_chunk, s, MASK_VALUE)
        m = jnp.max(s, axis=-1, keepdims=True)
        p = jnp.exp(s - m)
        denom = jnp.sum(p, axis=-1, keepdims=True)
        o = jnp.dot(p.astype(BF16), v_ref[0:q1, :], preferred_element_type=F32) / denom
        ms = jnp.mean(o * o, axis=-1, keepdims=True)
        o_ref[q0:q1, :] = (o * lax.rsqrt(ms + EPS) * g_ref[...]).astype(BF16)


def _out_proj_kernel(mc_ref, ma_ref, w_ref, x_ref, mod_ref, gn_ref, x1_o, h2_o, *, dconv):
    y = jnp.dot(mc_ref[...], w_ref[0:dconv, :], preferred_element_type=F32)
    y = y + jnp.dot(ma_ref[...], w_ref[dconv:, :], preferred_element_type=F32)
    x1 = x_ref[...] + mod_ref[2:3, :] * y
    x1_o[...] = x1
    ms = jnp.mean(x1 * x1, axis=-1, keepdims=True)
    h2 = x1 * lax.rsqrt(ms + EPS) * gn_ref[...] * (1.0 + mod_ref[4:5, :]) + mod_ref[3:4, :]
    h2_o[...] = h2.astype(BF16)


def _extract_top(work, index, count):
    rank = jnp.full(work.shape, float(count), F32)
    big = float(2 ** 20)
    vals = []
    for k in range(count):
        m = jnp.max(work, axis=0, keepdims=True)
        first = jnp.min(jnp.where(work == m, index, big), axis=0, keepdims=True)
        hit = index == first
        rank = jnp.where(hit, float(k), rank)
        work = jnp.where(hit, -jnp.inf, work)
        vals.append(m)
    return rank, vals


def _peer_select_kernel(h2_ref, wq_ref, keys_ref, r2_o, e2_o, n1_o, e1_o, s_ref, *, heads, topk):
    tm = h2_ref.shape[0]
    nk, dk = keys_ref.shape[1], keys_ref.shape[2]
    n_lt = tm // LANES
    q = jnp.dot(h2_ref[...], wq_ref[...], preferred_element_type=F32).astype(BF16)
    for hp in range(2 * heads):
        st = lax.dot_general(keys_ref[hp], q[:, hp * dk:(hp + 1) * dk], (((1,), (1,)), ((), ())),
                             preferred_element_type=F32)
        for lt in range(n_lt):
            s_ref[lt, hp] = st[:, lt * LANES:(lt + 1) * LANES]

    key_iota = lax.broadcasted_iota(jnp.int32, (nk, LANES), 0).astype(F32)
    sub_iota = lax.broadcasted_iota(jnp.int32, (SUBLANES, LANES), 0).astype(F32)
    half = topk // 2

    def body(idx, carry):
        lt = idx // heads
        h = idx % heads
        s1 = s_ref[lt, 2 * h]
        s2 = s_ref[lt, 2 * h + 1]
        rank1, a = _extract_top(s1, key_iota, topk)
        rank2, b = _extract_top(s2, key_iota, topk)
        b_lo = jnp.concatenate(b[:half], axis=0)
        b_hi = jnp.concatenate(b[half:], axis=0)
        a_hi = jnp.concatenate(a[half:], axis=0)
        tiles, flat, valid = [], [], []
        tiles.append(a[0] + b_lo); flat.append(sub_iota); valid.append(None)
        tiles.append(a[0] + b_hi); flat.append(sub_iota + half); valid.append(None)
        for k1 in range(1, half):
            tiles.append(a[k1] + b_lo)
            flat.append(sub_iota + k1 * topk)
            valid.append(sub_iota < (topk // (k1 + 1)))
        tiles.append(a_hi + b[0]); flat.append((sub_iota + half) * topk); valid.append(None)
        cand = jnp.concatenate(
            [t if ok is None else jnp.where(ok, t, -jnp.inf) for t, ok in zip(tiles, valid)], axis=0)
        flat_idx = jnp.concatenate(flat, axis=0)
        rank_c, best = _extract_top(cand, flat_idx, topk)
        taken = jnp.where(rank_c < float(topk), 1.0, 0.0)
        zsum = jnp.ones_like(best[0])
        for k in range(1, topk):
            zsum = zsum + jnp.exp(best[k] - best[0])
        inv_z = 1.0 / zsum
        counts = [jnp.sum(taken[0:2 * SUBLANES], axis=0, keepdims=True)]
        for k1 in range(1, half):
            counts.append(jnp.sum(taken[(k1 + 1) * SUBLANES:(k1 + 2) * SUBLANES], axis=0, keepdims=True))
        last = taken[(half + 1) * SUBLANES:(half + 2) * SUBLANES]
        n1 = jnp.zeros((nk, LANES), F32)
        for k1 in range(topk):
            cnt = counts[k1] if k1 < half else last[k1 - half:k1 - half + 1]
            n1 = jnp.where(rank1 == float(k1), cnt, n1)
        r2_o[lt, h] = rank2
        e2_o[lt, h] = jnp.where(rank2 < float(topk), jnp.exp(s2 - b[0]), 0.0)
        n1_o[lt, h] = n1
        e1_o[lt, h] = jnp.where(rank1 < float(topk), jnp.exp(s1 - a[0]) * inv_z, 0.0)
        return carry

    lax.fori_loop(0, n_lt * heads, body, 0)


def _peer_dense_kernel(h2_ref, u_ref, v_ref, r2_ref, e2_ref, n1_ref, e1_ref, o_ref, at_ref, act_ref, *, heads):
    j = pl.program_id(1)
    eb = u_ref.shape[0]
    tg = h2_ref.shape[0]
    rows_per_block = eb // LANES

    @pl.when(j == 0)
    def _():
        o_ref[...] = jnp.zeros(o_ref.shape, F32)

    at_ref[...] = lax.dot_general(u_ref[...].astype(BF16), h2_ref[...], (((1,), (1,)), ((), ())),
                                  preferred_element_type=F32)
    sqrt_half = math.sqrt(0.5)
    for c in range(rows_per_block):
        i1 = j * rows_per_block + c
        for ts in range(tg // LANES):
            gate = None
            for h in range(heads):
                n1 = n1_ref[ts, h, pl.ds(i1, 1), :]
                e1 = e1_ref[ts, h, pl.ds(i1, 1), :]
                term = jnp.where(r2_ref[ts, h] < n1, e2_ref[ts, h] * e1, 0.0)
                gate = term if gate is None else gate + term
            a = at_ref[c * LANES:(c + 1) * LANES, ts * LANES:(ts + 1) * LANES]
            act = 0.5 * a * (1.0 + lax.erf(a * sqrt_half)) * gate
            act_ref[ts * LANES:(ts + 1) * LANES, c * LANES:(c + 1) * LANES] = act.T.astype(BF16)
    o_ref[...] += jnp.dot(act_ref[...], v_ref[...].astype(BF16), preferred_element_type=F32)


def _final_kernel(x1_ref, p_ref, mod_ref, g_ref, o_ref):
    x2 = x1_ref[...] + mod_ref[5:6, :] * p_ref[...]
    ms = jnp.mean(x2 * x2, axis=-1, keepdims=True)
    o_ref[...] = x2 * lax.rsqrt(ms + EPS) * g_ref[...]


def _rope_tables(seq):
    inv = 1.0 / (ROPE_THETA ** (jnp.arange(0, QK_ROPE, 2, dtype=F32) / QK_ROPE))
    ang = jnp.arange(seq, dtype=F32)[:, None] * inv[None, :]
    pad = jnp.zeros((seq, LANES - QK_ROPE), F32)
    cos, sin = jnp.cos(ang), jnp.sin(ang)
    return jnp.concatenate([cos, cos, pad], axis=1), jnp.concatenate([sin, sin, pad], axis=1)


def _rotate_half_columns(w):
    half = w.shape[-1] // 2
    return jnp.concatenate([-w[..., half:], w[..., :half]], axis=-1)


def _pad_last(w, width):
    pad = [(0, 0)] * (w.ndim - 1) + [(0, width - w.shape[-1])]
    return jnp.pad(w, pad)


def _block(total, want):
    return want if total % want == 0 else total


def kernel(x, c, w_ada, b_ada, g_norm_mix, w_in, conv_w, g_q_lat, w_uq, g_kv_lat, w_ukv, g_out_conv, g_out_attn,
           w_out, g_norm_ffn, peer_w_q, peer_sub_keys, peer_u, peer_v, g_final):
    batch, seq, d = x.shape
    depth = w_ada.shape[0]
    assert depth == 1, "single trunk layer"
    tokens = batch * seq
    dconv = conv_w.shape[-1]
    qlora = g_q_lat.shape[-1]
    kvlora = g_kv_lat.shape[-1]
    heads = w_uq.shape[-1] // (QK_NOPE + QK_ROPE)
    d_attn = heads * V_HEAD
    n_keys = peer_sub_keys.shape[3]
    n_experts = peer_u.shape[1]
    assert QK_NOPE == LANES and V_HEAD == LANES and n_keys == LANES and dconv // CONV_GROUPS == LANES
    assert n_experts == n_keys * n_keys

    bp = -(-batch // SUBLANES) * SUBLANES
    c_pad = jnp.pad(c, ((0, bp - batch), (0, 0)))
    mod = _adaln(c_pad, w_ada[0], b_ada[0].reshape(1, -1), _block(N_MOD * d, 1024))
    mod = mod.reshape(bp, N_MOD, d)

    w_in0 = w_in[0]
    kb = 3 * dconv + qlora + kvlora
    w_kr = w_in0[:, kb:kb + QK_ROPE]
    w_in_p = jnp.concatenate(
        [w_in0[:, :kb], _pad_last(w_kr, LANES), _pad_last(_rotate_half_columns(w_kr), LANES)], axis=1).astype(BF16)
    wq3 = w_uq[0].reshape(qlora, heads, QK_NOPE + QK_ROPE)
    wq_rope = wq3[:, :, QK_NOPE:]
    w_uq_p = jnp.concatenate(
        [wq3[:, :, :QK_NOPE].reshape(qlora, heads * QK_NOPE),
         _pad_last(wq_rope, LANES).reshape(qlora, heads * LANES),
         _pad_last(_rotate_half_columns(wq_rope), LANES).reshape(qlora, heads * LANES)], axis=1).astype(BF16)
    wkv3 = w_ukv[0].reshape(kvlora, heads, QK_NOPE + V_HEAD)
    w_ukv_p = jnp.concatenate(
        [wkv3[:, :, :QK_NOPE].reshape(kvlora, heads * QK_NOPE),
         wkv3[:, :, QK_NOPE:].reshape(kvlora, heads * V_HEAD)], axis=1).astype(BF16)
    w_out_b = w_out[0].astype(BF16)
    w_pq_b = peer_w_q[0].astype(BF16)
    keys_b = peer_sub_keys[0].reshape(2 * PEER_HEADS, n_keys, -1).astype(BF16)
    cos_t, sin_t = _rope_tables(seq)

    x2d = x.reshape(tokens, d)
    row = lambda v: v.reshape(1, -1)

    tm = _block(seq, 256)
    nb_per_seq = seq // tm
    n_in = w_in_p.shape[1]
    hd = 2 * LANES
    tok_spec = lambda width: pl.BlockSpec((tm, width), lambda i: (i, 0))
    mod_spec = pl.BlockSpec((None, N_MOD, d), lambda i: (i // nb_per_seq, 0, 0))
    conv_m, q_all, k_all, v_all = pl.pallas_call(
        functools.partial(_mix_in_kernel, nb_per_seq=nb_per_seq, dconv=dconv, qlora=qlora, kvlora=kvlora,
                          heads=heads, scale=float((QK_NOPE + QK_ROPE) ** -0.5)),
        grid=(tokens // tm,),
        in_specs=[
            tok_spec(d), mod_spec, _const_spec((1, d)), _const_spec((d, n_in)), _const_spec((CONV_K, dconv)),
            _const_spec((1, dconv)), _const_spec((1, qlora)), _const_spec((1, kvlora)),
            _const_spec(w_uq_p.shape), _const_spec(w_ukv_p.shape),
            pl.BlockSpec((tm, LANES), lambda i: (i % nb_per_seq, 0)),
            pl.BlockSpec((tm, LANES), lambda i: (i % nb_per_seq, 0)),
        ],
        out_specs=[tok_spec(dconv), tok_spec(heads * hd), tok_spec(heads * hd), tok_spec(d_attn)],
        out_shape=[jax.ShapeDtypeStruct((tokens, dconv), BF16), jax.ShapeDtypeStruct((tokens, heads * hd), BF16),
                   jax.ShapeDtypeStruct((tokens, heads * hd), BF16), jax.ShapeDtypeStruct((tokens, d_attn), BF16)],
        scratch_shapes=[pltpu.VMEM((tm + SUBLANES, dconv), F32)],
        compiler_params=_cparams(("arbitrary",)),
        name="mix_in",
    )(x2d, mod, row(g_norm_mix[0]), w_in_p, conv_w[0], row(g_out_conv[0]), row(g_q_lat[0]), row(g_kv_lat[0]),
      w_uq_p, w_ukv_p, cos_t, sin_t)

    qb = _block(seq, 128)
    attn_m = pl.pallas_call(
        functools.partial(_attn_kernel, qb=qb),
        grid=(batch, heads),
        in_specs=[
            pl.BlockSpec((seq, hd), lambda b, h: (b, h)),
            pl.BlockSpec((seq, hd), lambda b, h: (b, h)),
            pl.BlockSpec((seq, V_HEAD), lambda b, h: (b, h)),
            pl.BlockSpec((1, V_HEAD), lambda b, h: (0, h)),
        ],
        out_specs=pl.BlockSpec((seq, V_HEAD), lambda b, h: (b, h)),
        out_shape=jax.ShapeDtypeStruct((tokens, d_attn), BF16),
        compiler_params=_cparams(("arbitrary", "arbitrary")),
        name="attention",
    )(q_all, k_all, v_all, row(g_out_attn[0]))

    x1, h2 = pl.pallas_call(
        functools.partial(_out_proj_kernel, dconv=dconv),
        grid=(tokens // tm,),
        in_specs=[tok_spec(dconv), tok_spec(d_attn), _const_spec(w_out_b.shape), tok_spec(d), mod_spec,
                  _const_spec((1, d))],
        out_specs=[tok_spec(d), tok_spec(d)],
        out_shape=[jax.ShapeDtypeStruct((tokens, d), F32), jax.ShapeDtypeStruct((tokens, d), BF16)],
        compiler_params=_cparams(("arbitrary",)),
        name="out_proj",
    )(conv_m, attn_m, w_out_b, x2d, mod, row(g_norm_ffn[0]))

    ts_sel = _block(tokens, 512)
    n_tiles = tokens // LANES
    tab = jax.ShapeDtypeStruct((n_tiles, PEER_HEADS, n_keys, LANES), F32)
    tab_spec = lambda rows: pl.BlockSpec((rows // LANES, PEER_HEADS, n_keys, LANES), lambda i, *_: (i, 0, 0, 0))
    r2_t, e2_t, n1_t, e1_t = pl.pallas_call(
        functools.partial(_peer_select_kernel, heads=PEER_HEADS, topk=PEER_TOPK),
        grid=(tokens // ts_sel,),
        in_specs=[pl.BlockSpec((ts_sel, d), lambda i: (i, 0)), _const_spec(w_pq_b.shape), _const_spec(keys_b.shape)],
        out_specs=[tab_spec(ts_sel)] * 4,
        out_shape=[tab] * 4,
        scratch_shapes=[pltpu.VMEM((ts_sel // LANES, 2 * PEER_HEADS, n_keys, LANES), F32)],
        compiler_params=_cparams(("arbitrary",)),
        name="peer_select",
    )(h2, w_pq_b, keys_b)

    tg = _block(tokens, 1024)
    eb = 256
    peer = pl.pallas_call(
        functools.partial(_peer_dense_kernel, heads=PEER_HEADS),
        grid=(tokens // tg, n_experts // eb),
        in_specs=[
            pl.BlockSpec((tg, d), lambda g, j: (g, 0)),
            pl.BlockSpec((eb, d), lambda g, j: (j, 0)),
            pl.BlockSpec((eb, d), lambda g, j: (j, 0)),
        ] + [pl.BlockSpec((tg // LANES, PEER_HEADS, n_keys, LANES), lambda g, j: (g, 0, 0, 0),
                          pipeline_mode=pl.Buffered(1))] * 4,
        out_specs=pl.BlockSpec((tg, d), lambda g, j: (g, 0)),
        out_shape=jax.ShapeDtypeStruct((tokens, d), F32),
        scratch_shapes=[pltpu.VMEM((eb, tg), F32), pltpu.VMEM((tg, eb), BF16)],
        compiler_params=_cparams(("arbitrary", "arbitrary")),
        name="peer_dense",
    )(h2, peer_u[0], peer_v[0], r2_t, e2_t, n1_t, e1_t)

    y = pl.pallas_call(
        _final_kernel,
        grid=(tokens // tm,),
        in_specs=[tok_spec(d), tok_spec(d), mod_spec, _const_spec((1, d))],
        out_specs=tok_spec(d),
        out_shape=jax.ShapeDtypeStruct((tokens, d), F32),
        compiler_params=_cparams(("arbitrary",)),
        name="final",
    )(x1, peer, mod, row(g_final))
    return y.reshape(batch, seq, d)
```
